```python
import math
import jax, jax.numpy as jnp
from jax import lax
import numpy as np

D_MODEL = 1024
BATCH = 32
SEQ = 256
DEPTH = 4
DEC_BATCH = 8
DEC_SEQ = 1024
PAST_LEN = 512

GRID_W = 64
CHUNK = 128
A_HEADS = 4
A_DH = 32
A_DV = 64
B_WIDTH = 256
B_GROUPS = 4
C_HEADS = 4
C_DK = 64
C_DV = 64
D_WIDTH = 256
CONV_W = 31
MIX_WIDTH = A_HEADS * A_DV + B_WIDTH + C_HEADS * C_DV + D_WIDTH
IN_SPLITS = (A_HEADS * 2 * A_DH, A_HEADS * 2 * A_DH, A_HEADS * A_DV,
             B_WIDTH, B_WIDTH,
             C_HEADS * C_DK, C_HEADS * C_DK, C_HEADS * C_DV, C_HEADS * C_DV,
             D_WIDTH, D_WIDTH)
IN_WIDTH = sum(IN_SPLITS)
PEER_HEADS = 8
PEER_DQ = 128
N_KEYS = 128
N_EXPERTS = N_KEYS * N_KEYS
PEER_TOPK = 16
PEER_BLOCK = 128
ROPE_BASE = 10000.0
LN_EPS = 1e-5
DEEPNORM_ALPHA = (2 * DEPTH) ** 0.25
DEEPNORM_BETA = (8 * DEPTH) ** -0.25
F32 = jnp.float32

kernel_name = 'hybrid_diffusion_step'


def layer_norm(x, g=None, b=None):
    xf = x.astype(F32)
    mu = xf.mean(-1, keepdims=True)
    var = jnp.square(xf - mu).mean(-1, keepdims=True)
    y = (xf - mu) * lax.rsqrt(var + LN_EPS)
    if g is not None:
        y = y * g.astype(F32) + b.astype(F32)
    return y.astype(x.dtype)


def rms_norm(x, g):
    xf = x.astype(F32)
    y = xf * lax.rsqrt(jnp.mean(xf * xf, -1, keepdims=True) + LN_EPS) * g.astype(F32)
    return y.astype(x.dtype)


def split_cols(z):
    parts, start = [], 0
    for w in IN_SPLITS:
        parts.append(z[..., start:start + w])
        start += w
    return parts


def grid_positions(n):
    rows = n // GRID_W
    r = jnp.broadcast_to(jnp.arange(rows)[:, None], (rows, GRID_W)).reshape(-1)
    c = jnp.broadcast_to(jnp.arange(GRID_W)[None, :], (rows, GRID_W)).reshape(-1)
    return r, c


def rope_1d(x, pos):
    half = x.shape[-1] // 2
    inv = ROPE_BASE ** (-jnp.arange(half, dtype=F32) / half)
    ang = pos.astype(F32)[:, None] * inv[None, :]
    cos = jnp.cos(ang)[None, :, None, None, :]
    sin = jnp.sin(ang)[None, :, None, None, :]
    xf = x.astype(F32)
    x1, x2 = xf[..., :half], xf[..., half:]
    return jnp.concatenate([x1 * cos - x2 * sin, x2 * cos + x1 * sin], -1).astype(x.dtype)


def axial_rope(x, rows, cols):
    b, n, h, _ = x.shape
    xs = x.reshape(b, n, h, 2, A_DH)
    q = A_DH // 2
    out = jnp.concatenate([rope_1d(xs[..., :q], rows), rope_1d(xs[..., q:], cols)], -1)
    return out.reshape(b, n, h, 2 * A_DH)


def lambda_init(l):
    return 0.8 - 0.6 * math.exp(-0.3 * l)


def diff_attention(q, k, v, lam):
    b, nq, h, _ = q.shape
    qb = q.reshape(b, nq // CHUNK, CHUNK, h, 2 * A_DH).transpose(1, 0, 2, 3, 4)
    kf = k.astype(F32)
    vf = v.astype(F32)
    k1, k2 = kf[..., :A_DH], kf[..., A_DH:]
    scale = A_DH ** -0.5

    def block(qblk):
        qf = qblk.astype(F32) * scale
        a1 = jax.nn.softmax(jnp.einsum('bqhd,bkhd->bhqk', qf[..., :A_DH], k1), axis=-1)
        a2 = jax.nn.softmax(jnp.einsum('bqhd,bkhd->bhqk', qf[..., A_DH:], k2), axis=-1)
        return jnp.einsum('bhqk,bkhd->bqhd', a1 - lam * a2, vf)

    o = lax.map(block, qb)
    return o.transpose(1, 0, 2, 3, 4).reshape(b, nq, h, A_DV)


def mixer_diff_attn(qa, ka, va, lam_p, subln_g, l, ctx_kv, pos):
    b, n, _ = qa.shape
    q = qa.reshape(b, n, A_HEADS, 2 * A_DH)
    k = ka.reshape(b, n, A_HEADS, 2 * A_DH)
    v = va.reshape(b, n, A_HEADS, A_DV)
    if ctx_kv is None:
        keys, vals = k, v
    else:
        q = axial_rope(q, pos[0], pos[1])
        keys = jnp.concatenate([ctx_kv[0].astype(k.dtype), axial_rope(k, pos[0], pos[1])], axis=1)
        vals = jnp.concatenate([ctx_kv[1].astype(v.dtype), v], axis=1)
    lp = lam_p.astype(F32)
    lam_i = lambda_init(l)
    lam = jnp.exp(jnp.sum(lp[0] * lp[1])) - jnp.exp(jnp.sum(lp[2] * lp[3])) + lam_i
    o = diff_attention(q, keys, vals, lam)
    o = rms_norm(o, subln_g) * (1.0 - lam_i)
    return o.reshape(b, n, A_HEADS * A_DV).astype(qa.dtype), (k, v)


def mixer_gmlp(ub, vb, ws, bs, ln_g, ln_b):
    b, n, _ = ub.shape
    u = jax.nn.gelu(ub)
    v = layer_norm(jax.nn.gelu(vb), ln_g, ln_b)
    vc = v.reshape(b, n // CHUNK, CHUNK, B_GROUPS, B_WIDTH // B_GROUPS)
    mixed = jnp.einsum('gpq,bcqgd->bcpgd', ws, vc) + bs.T[None, None, :, :, None]
    return u * mixed.reshape(b, n, B_WIDTH)


def retention_scan(q, k, v, log_gamma, s0):
    b, n, h, _ = q.shape
    nc = n // CHUNK

    def chunks(t):
        return t.reshape(b, nc, CHUNK, h, t.shape[-1]).transpose(1, 0, 3, 2, 4)

    pos = jnp.arange(CHUNK, dtype=F32)
    rel = pos[:, None] - pos[None, :]
    decay_mask = jnp.where(rel >= 0, jnp.exp(log_gamma[:, None, None] * jnp.maximum(rel, 0.0)), 0.0)
    q_decay = jnp.exp(log_gamma[:, None] * (pos + 1.0))[None, :, :, None]
    k_decay = jnp.exp(log_gamma[:, None] * (CHUNK - 1.0 - pos))[None, :, :, None]
    chunk_decay = jnp.exp(log_gamma * CHUNK)[None, :, None, None]

    def step(state, inp):
        qc, kc, vc = inp
        scores = jnp.einsum('bhtd,bhsd->bhts', qc, kc) * decay_mask
        out = (jnp.einsum('bhts,bhsv->bhtv', scores, vc)
               + jnp.einsum('bhtd,bhdv->bhtv', qc * q_decay, state))
        state = state * chunk_decay + jnp.einsum('bhsd,bhsv->bhdv', kc * k_decay, vc)
        return state, out

    s_final, out = lax.scan(step, s0.astype(F32), (chunks(q), chunks(k), chunks(v)))
    out = out.transpose(1, 0, 3, 2, 4).reshape(b, n, h, v.shape[-1])
    return out, s_final


def mixer_retention(qc, kc, vc, gc, decay_logit, s0_f, s0_b):
    b, n, _ = qc.shape
    q = qc.reshape(b, n, C_HEADS, C_DK).astype(F32)
    k = kc.reshape(b, n, C_HEADS, C_DK).astype(F32) * (C_DK ** -0.5)
    v = vc.reshape(b, n, C_HEADS, C_DV).astype(F32)
    lg = jax.nn.log_sigmoid(decay_logit.astype(F32))
    o_f, s_f = retention_scan(q, k, v, lg[0], s0_f)
    o_b, s_b = retention_scan(q[:, ::-1], k[:, ::-1], v[:, ::-1], lg[1], s0_b)
    o = layer_norm(o_f + o_b[:, ::-1])
    o = o.reshape(b, n, C_HEADS * C_DV) * jax.nn.silu(gc.astype(F32))
    return o.astype(qc.dtype), (s_f, s_b)


def mixer_conv(ad, bd, w, bias, ln_g, ln_b):
    h = ad * jax.nn.sigmoid(bd)
    y = lax.conv_general_dilated(h, w[:, None, :].astype(h.dtype), (1,),
                                 [(CONV_W // 2, CONV_W // 2)],
                                 dimension_numbers=('NWC', 'WIO', 'NWC'),
                                 feature_group_count=D_WIDTH)
    y = layer_norm(y + bias.astype(y.dtype), ln_g, ln_b)
    return jax.nn.silu(y)


def peer(h, wq, keys, U, V):
    b, n, d = h.shape
    T = b * n
    x = h.reshape(T, d)
    q = (x @ wq).reshape(T, PEER_HEADS, 2, PEER_DQ // 2).astype(F32)
    s = jnp.einsum('thpd,pkd->thpk', q, keys.astype(F32))
    s1, i1 = lax.top_k(s[:, :, 0], PEER_TOPK)
    s2, i2 = lax.top_k(s[:, :, 1], PEER_TOPK)
    cand = (s1[..., :, None] + s2[..., None, :]).reshape(T, PEER_HEADS, PEER_TOPK * PEER_TOPK)
    cidx = (i1[..., :, None] * N_KEYS + i2[..., None, :]).reshape(T, PEER_HEADS, PEER_TOPK * PEER_TOPK)
    top_s, top_j = lax.top_k(cand, PEER_TOPK)
    eidx = jnp.take_along_axis(cidx, top_j, axis=-1)
    gate = jax.nn.softmax(top_s, axis=-1)
    E = PEER_HEADS * PEER_TOPK
    nb = T // PEER_BLOCK
    xb = x.reshape(nb, PEER_BLOCK, d)
    ib = eidx.reshape(nb, PEER_BLOCK, E)
    gb = gate.reshape(nb, PEER_BLOCK, E)

    def block(args):
        xt, it, gt = args
        u = jnp.take(U, it, axis=0)
        act = jax.nn.gelu(jnp.einsum('td,ted->te', xt, u).astype(F32))
        v = jnp.take(V, it, axis=0)
        return jnp.einsum('te,ted->td', (gt * act).astype(v.dtype), v)

    y = lax.map(block, (xb, ib, gb))
    return y.reshape(b, n, d).astype(h.dtype)


def trunk_layer(x, cvec, p, l, ctx=None):
    b, n, _ = x.shape
    mod = jax.nn.silu(cvec) @ p['w_mod'] + p['b_mod']
    sh1, sc1, g1, sh2, sc2, g2 = [m[:, None, :] for m in jnp.split(mod, 6, axis=-1)]
    h = x * (1.0 + sc1) + sh1
    z = h @ p['w_in']
    qa, ka, va, ub, vb, qc, kc, vc, gc, ad, bd = split_cols(z)
    is_ctx = ctx is None
    if is_ctx:
        oa, kv = mixer_diff_attn(qa, ka, va, p['diff_lambda'], p['diff_subln_g'], l, None, None)
        s0_f = jnp.zeros((b, C_HEADS, C_DK, C_DV), F32)
        s0_b = s0_f
    else:
        oa, kv = mixer_diff_attn(qa, ka, va, p['diff_lambda'], p['diff_subln_g'], l,
                                 (ctx[0], ctx[1]), grid_positions(n))
        s0_f, s0_b = ctx[2], ctx[3]
    ob = mixer_gmlp(ub, vb, p['gmlp_ws'], p['gmlp_bs'], p['gmlp_ln_g'], p['gmlp_ln_b'])
    oc, (s_f, s_b) = mixer_retention(qc, kc, vc, gc, p['ret_decay'], s0_f, s0_b)
    od = mixer_conv(ad, bd, p['conv_w'], p['conv_b'], p['conv_ln_g'], p['conv_ln_b'])
    mix = jnp.concatenate([oa, ob, oc, od], axis=-1) @ p['w_out']
    x = layer_norm(DEEPNORM_ALPHA * x + g1 * mix, p['ln1_g'], p['ln1_b'])
    h = x * (1.0 + sc2) + sh2
    f = peer(h, p['peer_wq'], p['peer_keys'], p['peer_u'], p['peer_v'])
    x = layer_norm(DEEPNORM_ALPHA * x + g2 * f, p['ln2_g'], p['ln2_b'])
    if is_ctx:
        return x, (kv[0], kv[1], s_f, s_b)
    return x, None


def setup_inputs(seed: int = 0) -> dict:
    key = jax.random.key(seed)
    ks = jax.random.split(key, 30)

    def nrm(k, shape, s):
        return jax.random.normal(k, shape, F32) * s

    gam = 1.0 - 2.0 ** (-5.0 - np.arange(C_HEADS))
    decay_logit = jnp.asarray(np.log(gam / (1.0 - gam)).astype(np.float32))
    return {
        'x_prompt': nrm(ks[0], (BATCH, SEQ, D_MODEL), 1.0),
        'x_sample': nrm(ks[1], (DEC_BATCH, DEC_SEQ, D_MODEL), 1.0),
        'cache_k': nrm(ks[2], (DEC_BATCH, DEPTH, PAST_LEN, A_HEADS, 2 * A_DH), 1.0),
        'cache_v': nrm(ks[3], (DEC_BATCH, DEPTH, PAST_LEN, A_HEADS, A_DV), 1.0),
        'state_ret': nrm(ks[4], (DEC_BATCH, DEPTH, 2, C_HEADS, C_DK, C_DV), 1.0),
        'c': nrm(ks[5], (DEC_BATCH, D_MODEL), 1.0),
        'c_ctx': nrm(ks[6], (D_MODEL,), 1.0),
        'w_mod': nrm(ks[7], (DEPTH, D_MODEL, 6 * D_MODEL), 0.5 * D_MODEL ** -0.5),
        'b_mod': nrm(ks[8], (DEPTH, 6 * D_MODEL), 0.01),
        'w_in': nrm(ks[9], (DEPTH, D_MODEL, IN_WIDTH), D_MODEL ** -0.5),
        'diff_lambda': nrm(ks[10], (DEPTH, 4, A_DH), 0.1),
        'diff_subln_g': 1.0 + nrm(ks[11], (DEPTH, A_DV), 0.01),
        'gmlp_ln_g': 1.0 + nrm(ks[12], (DEPTH, B_WIDTH), 0.01),
        'gmlp_ln_b': nrm(ks[13], (DEPTH, B_WIDTH), 0.01),
        'gmlp_ws': nrm(ks[14], (DEPTH, B_GROUPS, CHUNK, CHUNK), 0.5 * CHUNK ** -0.5),
        'gmlp_bs': 1.0 + nrm(ks[15], (DEPTH, B_GROUPS, CHUNK), 0.01),
        'ret_decay': decay_logit[None, None, :] + nrm(ks[16], (DEPTH, 2, C_HEADS), 0.01),
        'conv_w': nrm(ks[17], (DEPTH, CONV_W, D_WIDTH), CONV_W ** -0.5),
        'conv_b': nrm(ks[18], (DEPTH, D_WIDTH), 0.01),
        'conv_ln_g': 1.0 + nrm(ks[19], (DEPTH, D_WIDTH), 0.01),
        'conv_ln_b': nrm(ks[20], (DEPTH, D_WIDTH), 0.01),
        'w_out': nrm(ks[21], (DEPTH, MIX_WIDTH, D_MODEL), DEEPNORM_BETA * MIX_WIDTH ** -0.5),
        'ln1_g': 1.0 + nrm(ks[22], (DEPTH, D_MODEL), 0.01),
        'ln1_b': nrm(ks[23], (DEPTH, D_MODEL), 0.01),
        'ln2_g': 1.0 + nrm(ks[24], (DEPTH, D_MODEL), 0.01),
        'ln2_b': nrm(ks[25], (DEPTH, D_MODEL), 0.01),
        'peer_wq': nrm(ks[26], (DEPTH, D_MODEL, PEER_HEADS * PEER_DQ), D_MODEL ** -0.5),
        'peer_keys': nrm(ks[27], (DEPTH, 2, N_KEYS, PEER_DQ // 2), (PEER_DQ // 2) ** -0.5),
        'peer_u': nrm(ks[28], (DEPTH, N_EXPERTS, D_MODEL), D_MODEL ** -0.5),
        'peer_v': nrm(ks[29], (DEPTH, N_EXPERTS, D_MODEL), DEEPNORM_BETA * PEER_HEADS ** -0.5),
    }


def reference(x_prompt, x_sample, cache_k, cache_v, state_ret, c, c_ctx,
              w_mod, b_mod, w_in, diff_lambda, diff_subln_g,
              gmlp_ln_g, gmlp_ln_b, gmlp_ws, gmlp_bs, ret_decay,
              conv_w, conv_b, conv_ln_g, conv_ln_b, w_out,
              ln1_g, ln1_b, ln2_g, ln2_b,
              peer_wq, peer_keys, peer_u, peer_v):
    ctx_cond = c_ctx[None, :]
    xp, xs = x_prompt, x_sample
    ks_out, vs_out, ss_out = [], [], []
    for l in range(DEPTH):
        p = {
            'w_mod': w_mod[l], 'b_mod': b_mod[l], 'w_in': w_in[l],
            'diff_lambda': diff_lambda[l], 'diff_subln_g': diff_subln_g[l],
            'gmlp_ln_g': gmlp_ln_g[l], 'gmlp_ln_b': gmlp_ln_b[l],
            'gmlp_ws': gmlp_ws[l], 'gmlp_bs': gmlp_bs[l], 'ret_decay': ret_decay[l],
            'conv_w': conv_w[l], 'conv_b': conv_b[l],
            'conv_ln_g': conv_ln_g[l], 'conv_ln_b': conv_ln_b[l], 'w_out': w_out[l],
            'ln1_g': ln1_g[l], 'ln1_b': ln1_b[l], 'ln2_g': ln2_g[l], 'ln2_b': ln2_b[l],
            'peer_wq': peer_wq[l], 'peer_keys': peer_keys[l],
            'peer_u': peer_u[l], 'peer_v': peer_v[l],
        }
        xp, (k_l, v_l, sf_l, sb_l) = trunk_layer(xp, ctx_cond, p, l)
        ks_out.append(k_l)
        vs_out.append(v_l)
        ss_out.append(jnp.stack([sf_l, sb_l], axis=1))
        xs, _ = trunk_layer(xs, c, p, l,
                            (cache_k[:, l], cache_v[:, l], state_ret[:, l, 0], state_ret[:, l, 1]))
    new_k = jnp.stack(ks_out, axis=1)
    new_v = jnp.stack(vs_out, axis=1)
    new_state_ret = jnp.stack(ss_out, axis=1)
    return (xp, xs, new_k, new_v, new_state_ret)
```

```python
import functools
import math

import numpy as np
import jax
import jax.numpy as jnp
from jax import lax
from jax.experimental import pallas as pl
from jax.experimental.pallas import tpu as pltpu

F32 = jnp.float32
BF16 = jnp.bfloat16

D_MODEL = 1024
DEPTH = 4
GRID_W = 64
CHUNK = 128
A_HEADS = 4
A_DH = 32
A_DV = 64
HEAD_W = 64
GROUP_W = 256
N_GROUPS = 11
IN_WIDTH = N_GROUPS * GROUP_W
CONV_W = 31
CONV_PAD = 16
PAST_LEN = 512
PEER_HEADS = 8
N_KEYS = 128
N_EXPERTS = N_KEYS * N_KEYS
PEER_TOPK = 16
ROPE_BASE = 10000.0
LN_EPS = 1e-5
DEEPNORM_ALPHA = (2 * DEPTH) ** 0.25

ROW_BLOCK = 256
PEER_TOKENS = 512
PEER_LANES = 256
PEER_SUB = PEER_TOKENS // PEER_LANES
PEER_ROWS = 4
PEER_TILE = PEER_ROWS * N_KEYS
VMEM_LIMIT = 48 * 1024 * 1024

NN = (((1,), (0,)), ((), ()))
NT = (((1,), (1,)), ((), ()))


def _dot(a, b, dims=NN):
    return lax.dot_general(a, b, dims, preferred_element_type=F32)


def _split(a):
    hi = a.astype(BF16)
    lo = (a - hi.astype(F32)).astype(BF16)
    return hi, lo


def _dot3(a, b, dims=NN):
    ah, al = _split(a)
    bh, bl = _split(b)
    return _dot(ah, bh, dims) + _dot(al, bh, dims) + _dot(ah, bl, dims)


def _dot2(a, b_exact, dims=NN):
    ah, al = _split(a)
    bb = b_exact.astype(BF16)
    return _dot(ah, bb, dims) + _dot(al, bb, dims)


def _sigmoid(x):
    return 1.0 / (1.0 + jnp.exp(-x))


def _gelu(x):
    return 0.5 * x * (1.0 + jnp.tanh(0.7978845608028654 * (x + 0.044715 * (x * x * x))))


def _log_sigmoid(x):
    return -(jnp.maximum(-x, 0.0) + jnp.log1p(jnp.exp(-jnp.abs(x))))


def _layer_norm(x, g=None, b=None):
    mu = jnp.mean(x, axis=-1, keepdims=True)
    xc = x - mu
    var = jnp.mean(xc * xc, axis=-1, keepdims=True)
    y = xc * lax.rsqrt(var + LN_EPS)
    if g is not None:
        y = y * g + b
    return y


def _lane_group_mask(width, group, g):
    lane = lax.broadcasted_iota(jnp.int32, (1, width), 1)
    return jnp.where((lane >= g * group) & (lane < (g + 1) * group), 1.0, 0.0).astype(F32)


def _block_diag(width, group, value):
    r = lax.broadcasted_iota(jnp.int32, (width, width), 0) // group
    c = lax.broadcasted_iota(jnp.int32, (width, width), 1) // group
    return jnp.where(r == c, value, 0.0).astype(F32)


def _params(*sem):
    return pltpu.CompilerParams(dimension_semantics=sem, vmem_limit_bytes=VMEM_LIMIT)


def _mod_kernel(c_ref, w_ref, b_ref, o_ref):
    c = c_ref[...]
    o_ref[0] = _dot3(c * _sigmoid(c), w_ref[0]) + b_ref[0]


def _mod_call(cond, w_mod, b_mod):
    rows = cond.shape[0]
    return pl.pallas_call(
        _mod_kernel,
        grid=(DEPTH, 6),
        in_specs=[
            pl.BlockSpec((rows, D_MODEL), lambda l, n: (0, 0)),
            pl.BlockSpec((1, D_MODEL, D_MODEL), lambda l, n: (l, 0, n)),
            pl.BlockSpec((1, 1, D_MODEL), lambda l, n: (l, 0, n)),
        ],
        out_specs=pl.BlockSpec((1, rows, D_MODEL), lambda l, n: (l, 0, n)),
        out_shape=jax.ShapeDtypeStruct((DEPTH, rows, 6 * D_MODEL), F32),
        compiler_params=_params("parallel", "parallel"),
        name="adaln_mod",
    )(cond, w_mod, b_mod.reshape(DEPTH, 1, 6 * D_MODEL))


def _inproj_kernel(x_ref, mod_ref, w_ref, z_ref):
    h = x_ref[...] * (1.0 + mod_ref[1:2, :]) + mod_ref[0:1, :]
    z_ref[...] = _dot(h.astype(BF16), w_ref[...])


def _mod_index(per_batch):
    if per_batch:
        return lambda b, r: (b, 0, 0)
    return lambda b, r: (0, 0, 0)


def _inproj_call(x, mod, w_in, per_batch):
    B, n, _ = x.shape
    return pl.pallas_call(
        _inproj_kernel,
        grid=(B, n // ROW_BLOCK),
        in_specs=[
            pl.BlockSpec((None, ROW_BLOCK, D_MODEL), lambda b, r: (b, r, 0)),
            pl.BlockSpec((None, 6, D_MODEL), _mod_index(per_batch)),
            pl.BlockSpec((D_MODEL, IN_WIDTH), lambda b, r: (0, 0)),
        ],
        out_specs=pl.BlockSpec((None, ROW_BLOCK, IN_WIDTH), lambda b, r: (b, r, 0)),
        out_shape=jax.ShapeDtypeStruct((B, n, IN_WIDTH), F32),
        compiler_params=_params("parallel", "parallel"),
        name="in_proj",
    )(x, mod, w_in)


def _rope(x, cos, sin_next, sin_prev):
    width = x.shape[-1]
    nxt = pltpu.roll(x, width - 8, 1)
    prv = pltpu.roll(x, 8, 1)
    return x * cos + nxt * sin_next + prv * sin_prev


def _diff_lambda(lam_ref, li):
    lp = lam_ref[...]
    t1 = jnp.sum(lp[0:1, :] * lp[1:2, :], axis=-1, keepdims=True)
    t2 = jnp.sum(lp[2:3, :] * lp[3:4, :], axis=-1, keepdims=True)
    return jnp.exp(t1) - jnp.exp(t2) + li


def _attn_core(q, keys, vals, lam, li, sg):
    qs = q * (A_DH ** -0.5)
    out = jnp.zeros(q.shape, F32)
    for h in range(A_HEADS):
        probs = []
        for p in range(2):
            lane_mask = _lane_group_mask(GROUP_W, A_DH, 2 * h + p)
            logits = _dot((qs * lane_mask).astype(BF16), keys, NT)
            mx = jnp.max(logits, axis=-1, keepdims=True)
            e = jnp.exp(logits - mx)
            probs.append(e / jnp.sum(e, axis=-1, keepdims=True))
        a = (probs[0] - lam * probs[1]).astype(BF16)
        head_mask = _lane_group_mask(GROUP_W, HEAD_W, h).astype(BF16)
        out = out + _dot(a, vals * head_mask)
    ms = _dot2(out * out, _block_diag(GROUP_W, HEAD_W, 1.0 / HEAD_W))
    return out * lax.rsqrt(ms + LN_EPS) * sg * (1.0 - li)


def _attn_ctx_kernel(q_ref, k_ref, v_ref, lam_ref, li_ref, sg_ref, o_ref):
    li = li_ref[0:1, 0:1]
    lam = _diff_lambda(lam_ref, li)
    y = _attn_core(q_ref[...], k_ref[...].astype(BF16), v_ref[...].astype(BF16),
                   lam, li, sg_ref[...])
    o_ref[...] = y.astype(BF16)


def _attn_lat_kernel(q_ref, k_ref, v_ref, ck_ref, cv_ref, cq_ref, sqn_ref, sqp_ref,
                     ckk_ref, skn_ref, skp_ref, lam_ref, li_ref, sg_ref, o_ref,
                     keys_ref, vals_ref):
    @pl.when(pl.program_id(1) == 0)
    def _():
        keys_ref[0:PAST_LEN, :] = ck_ref[...].astype(BF16)
        vals_ref[0:PAST_LEN, :] = cv_ref[...].astype(BF16)
        kr = _rope(k_ref[...], ckk_ref[...], skn_ref[...], skp_ref[...])
        keys_ref[PAST_LEN:, :] = kr.astype(BF16)
        vals_ref[PAST_LEN:, :] = v_ref[...].astype(BF16)

    li = li_ref[0:1, 0:1]
    lam = _diff_lambda(lam_ref, li)
    q = _rope(q_ref[...], cq_ref[...], sqn_ref[...], sqp_ref[...])
    y = _attn_core(q, keys_ref[...], vals_ref[...], lam, li, sg_ref[...])
    o_ref[...] = y.astype(BF16)


def _zcol(n, g):
    return pl.BlockSpec((None, n, GROUP_W), lambda b, *_: (b, 0, g))


def _small(shape):
    nd = len(shape)
    return pl.BlockSpec(shape, lambda *_: (0,) * nd)


def _attn_ctx_call(z, lam_p, li, sg):
    B, n, _ = z.shape
    return pl.pallas_call(
        _attn_ctx_kernel,
        grid=(B,),
        in_specs=[_zcol(n, 0), _zcol(n, 1), _zcol(n, 2),
                  _small((4, A_DH)), _small((1, 128)), _small((1, GROUP_W))],
        out_specs=pl.BlockSpec((None, n, GROUP_W), lambda b: (b, 0, 0)),
        out_shape=jax.ShapeDtypeStruct((B, n, GROUP_W), BF16),
        compiler_params=_params("parallel"),
        name="diff_attn_ctx",
    )(z, z, z, lam_p, li, sg)


def _rope_tables(n):
    t = np.arange(n)
    pos = np.stack([t // GRID_W, t % GRID_W], axis=0).astype(np.float64)
    j = np.arange(GROUP_W)
    d = j % A_DH
    axis = d // (A_DH // 2)
    dd = d % (A_DH // 2)
    half = A_DH // 4
    inv = ROPE_BASE ** (-(dd % half).astype(np.float64) / half)
    ang = pos[axis, :].T * inv[None, :]
    first = (dd < half)[None, :]
    cos = np.cos(ang)
    sin = np.sin(ang)
    return (jnp.asarray(cos, F32), jnp.asarray(np.where(first, -sin, 0.0), F32),
            jnp.asarray(np.where(first, 0.0, sin), F32))


def _attn_lat_call(z, cache_k, cache_v, l, lam_p, li, sg):
    B, n, _ = z.shape
    cos, sin_next, sin_prev = _rope_tables(n)
    nq = n // ROW_BLOCK
    qtab = pl.BlockSpec((ROW_BLOCK, GROUP_W), lambda b, r: (r, 0))
    ktab = pl.BlockSpec((n, GROUP_W), lambda b, r: (0, 0))
    cache = pl.BlockSpec((None, None, PAST_LEN, GROUP_W), lambda b, r: (b, l, 0, 0))
    return pl.pallas_call(
        _attn_lat_kernel,
        grid=(B, nq),
        in_specs=[pl.BlockSpec((None, ROW_BLOCK, GROUP_W), lambda b, r: (b, r, 0)),
                  _zcol(n, 1), _zcol(n, 2), cache, cache,
                  qtab, qtab, qtab, ktab, ktab, ktab,
                  _small((4, A_DH)), _small((1, 128)), _small((1, GROUP_W))],
        out_specs=pl.BlockSpec((None, ROW_BLOCK, GROUP_W), lambda b, r: (b, r, 0)),
        out_shape=jax.ShapeDtypeStruct((B, n, GROUP_W), BF16),
        scratch_shapes=[pltpu.VMEM((PAST_LEN + n, GROUP_W), BF16),
                        pltpu.VMEM((PAST_LEN + n, GROUP_W), BF16)],
        compiler_params=_params("parallel", "arbitrary"),
        name="diff_attn_latent",
    )(z, z, z, cache_k, cache_v, cos, sin_next, sin_prev, cos, sin_next, sin_prev,
      lam_p, li, sg)


def _gmlp_kernel(u_ref, v_ref, g_ref, b_ref, ws_ref, bias_ref, o_ref):
    u = _gelu(u_ref[...])
    v = _layer_norm(_gelu(v_ref[...]), g_ref[...], b_ref[...])
    groups = ws_ref.shape[0]
    masks = [_lane_group_mask(GROUP_W, GROUP_W // groups, g) for g in range(groups)]
    for c in range(ROW_BLOCK // CHUNK):
        rows = slice(c * CHUNK, (c + 1) * CHUNK)
        vc = v[rows, :]
        mixed = bias_ref[...]
        for g in range(groups):
            mixed = mixed + _dot(ws_ref[g].astype(BF16), (vc * masks[g]).astype(BF16))
        o_ref[rows, :] = (u[rows, :] * mixed).astype(BF16)


def _gmlp_call(z, ln_g, ln_b, ws, bias):
    B, n, _ = z.shape
    col = lambda g: pl.BlockSpec((None, ROW_BLOCK, GROUP_W), lambda b, r: (b, r, g))
    return pl.pallas_call(
        _gmlp_kernel,
        grid=(B, n // ROW_BLOCK),
        in_specs=[col(3), col(4), _small((1, GROUP_W)), _small((1, GROUP_W)),
                  _small(ws.shape), _small((CHUNK, GROUP_W))],
        out_specs=pl.BlockSpec((None, ROW_BLOCK, GROUP_W), lambda b, r: (b, r, 0)),
        out_shape=jax.ShapeDtypeStruct((B, n, GROUP_W), BF16),
        compiler_params=_params("parallel", "parallel"),
        name="gmlp",
    )(z, z, ln_g, ln_b, ws, bias)


def _ret_body(q_ref, k_ref, v_ref, g_ref, dr_ref, dc_ref, s0_ref, o_ref, st_out_ref,
              acc_ref, st_ref, *, n):
    nc = n // CHUNK
    lg_rows = _log_sigmoid(dr_ref[...])
    lg_cols = _log_sigmoid(dc_ref[...])
    ti = lax.broadcasted_iota(jnp.int32, (CHUNK, CHUNK), 0)
    si = lax.broadcasted_iota(jnp.int32, (CHUNK, CHUNK), 1)
    pos = lax.broadcasted_iota(jnp.int32, (CHUNK, GROUP_W), 0).astype(F32)
    head_masks = [_lane_group_mask(GROUP_W, HEAD_W, h) for h in range(A_HEADS)]
    diag = _block_diag(GROUP_W, HEAD_W, 1.0)

    for d in range(2):
        rel = (ti - si) if d == 0 else (si - ti)
        relf = jnp.maximum(rel, 0).astype(F32)
        decay = [jnp.where(rel >= 0, jnp.exp(lg_rows[d * 4 + h:d * 4 + h + 1, :] * relf), 0.0)
                 for h in range(A_HEADS)]
        lgc = lg_cols[d:d + 1, :]
        if d == 0:
            q_decay = jnp.exp(lgc * (pos + 1.0))
            k_decay = jnp.exp(lgc * (CHUNK - 1.0 - pos))
        else:
            q_decay = jnp.exp(lgc * (CHUNK - pos))
            k_decay = jnp.exp(lgc * pos)
        chunk_decay = jnp.exp(lgc * float(CHUNK))
        if s0_ref is None:
            st_ref[...] = jnp.zeros((GROUP_W, GROUP_W), F32)
        else:
            st_ref[...] = s0_ref[d]

        def body(i, carry):
            c = i if d == 0 else nc - 1 - i
            rows = pl.ds(pl.multiple_of(c * CHUNK, CHUNK), CHUNK)
            qc = q_ref[rows, :]
            kc = k_ref[rows, :] * (HEAD_W ** -0.5)
            vc = v_ref[rows, :]
            st = st_ref[...]
            scores = [_dot3(qc * head_masks[h], kc, NT) * decay[h] for h in range(A_HEADS)]
            p_all = jnp.concatenate(scores, axis=1)
            v_all = jnp.concatenate([vc * head_masks[h] for h in range(A_HEADS)], axis=0)
            out = _dot3(p_all, v_all) + _dot3(qc * q_decay, st)
            kv = _dot3((kc * k_decay).T, vc)
            st_ref[...] = st * chunk_decay + kv * diag
            if d == 0:
                acc_ref[rows, :] = out
            else:
                acc_ref[rows, :] = acc_ref[rows, :] + out
            return carry

        lax.fori_loop(0, nc, body, 0)
        if st_out_ref is not None:
            st_out_ref[d] = st_ref[...]

    avg = _block_diag(GROUP_W, HEAD_W, 1.0 / HEAD_W)
    for r in range(n // ROW_BLOCK):
        rows = slice(r * ROW_BLOCK, (r + 1) * ROW_BLOCK)
        o = acc_ref[rows, :]
        xc = o - _dot2(o, avg)
        y = xc * lax.rsqrt(_dot2(xc * xc, avg) + LN_EPS)
        g = g_ref[rows, :]
        o_ref[rows, :] = (y * (g * _sigmoid(g))).astype(BF16)


def _ret_ctx_kernel(q_ref, k_ref, v_ref, g_ref, dr_ref, dc_ref, o_ref, st_out_ref,
                    acc_ref, st_ref, *, n):
    _ret_body(q_ref, k_ref, v_ref, g_ref, dr_ref, dc_ref, None, o_ref, st_out_ref,
              acc_ref, st_ref, n=n)


def _ret_lat_kernel(q_ref, k_ref, v_ref, g_ref, dr_ref, dc_ref, s0_ref, o_ref,
                    acc_ref, st_ref, *, n):
    _ret_body(q_ref, k_ref, v_ref, g_ref, dr_ref, dc_ref, s0_ref, o_ref, None,
              acc_ref, st_ref, n=n)


def _ret_call(z, dec_rows, dec_cols, s0=None):
    B, n, _ = z.shape
    in_specs = [_zcol(n, 5), _zcol(n, 6), _zcol(n, 7), _zcol(n, 8),
                _small((8, 128)), _small((2, GROUP_W))]
    o_spec = pl.BlockSpec((None, n, GROUP_W), lambda b: (b, 0, 0))
    o_shape = jax.ShapeDtypeStruct((B, n, GROUP_W), BF16)
    st_spec = pl.BlockSpec((None, 2, GROUP_W, GROUP_W), lambda b: (b, 0, 0, 0))
    scratch = [pltpu.VMEM((n, GROUP_W), F32), pltpu.VMEM((GROUP_W, GROUP_W), F32)]
    if s0 is None:
        return pl.pallas_call(
            functools.partial(_ret_ctx_kernel, n=n),
            grid=(B,),
            in_specs=in_specs,
            out_specs=(o_spec, st_spec),
            out_shape=(o_shape, jax.ShapeDtypeStruct((B, 2, GROUP_W, GROUP_W), F32)),
            scratch_shapes=scratch,
            compiler_params=_params("parallel"),
            name="retention_ctx",
        )(z, z, z, z, dec_rows, dec_cols)
    return pl.pallas_call(
        functools.partial(_ret_lat_kernel, n=n),
        grid=(B,),
        in_specs=in_specs + [st_spec],
        out_specs=o_spec,
        out_shape=o_shape,
        scratch_shapes=scratch,
        compiler_params=_params("parallel"),
        name="retention_latent",
    )(z, z, z, z, dec_rows, dec_cols, s0), None


def _conv_kernel(a_ref, b_ref, w_ref, cb_ref, g_ref, bb_ref, o_ref, pad_ref, *, n):
    zeros = jnp.zeros((CONV_PAD, GROUP_W), F32)
    pad_ref[0:CONV_PAD, :] = zeros
    pad_ref[CONV_PAD + n:CONV_PAD + n + CONV_PAD, :] = zeros
    for r in range(n // ROW_BLOCK):
        rows = slice(r * ROW_BLOCK, (r + 1) * ROW_BLOCK)
        pad_ref[CONV_PAD + r * ROW_BLOCK:CONV_PAD + (r + 1) * ROW_BLOCK, :] = (
            a_ref[rows, :] * _sigmoid(b_ref[rows, :]))
    shift = CONV_PAD - CONV_W // 2
    for r in range(n // CHUNK):
        acc = jnp.zeros((CHUNK, GROUP_W), F32)
        for j in range(CONV_W):
            start = r * CHUNK + j + shift
            acc = acc + pad_ref[start:start + CHUNK, :] * w_ref[j:j + 1, :]
        y = _layer_norm(acc + cb_ref[...], g_ref[...], bb_ref[...])
        o_ref[r * CHUNK:(r + 1) * CHUNK, :] = (y * _sigmoid(y)).astype(BF16)


def _conv_call(z, w, cb, ln_g, ln_b):
    B, n, _ = z.shape
    return pl.pallas_call(
        functools.partial(_conv_kernel, n=n),
        grid=(B,),
        in_specs=[_zcol(n, 9), _zcol(n, 10), _small((CONV_W, GROUP_W)),
                  _small((1, GROUP_W)), _small((1, GROUP_W)), _small((1, GROUP_W))],
        out_specs=pl.BlockSpec((None, n, GROUP_W), lambda b: (b, 0, 0)),
        out_shape=jax.ShapeDtypeStruct((B, n, GROUP_W), BF16),
        scratch_shapes=[pltpu.VMEM((n + 2 * CONV_PAD, GROUP_W), F32)],
        compiler_params=_params("parallel"),
        name="conformer_conv",
    )(z, z, w, cb, ln_g, ln_b)


def _outproj_kernel(oa_ref, ob_ref, oc_ref, od_ref, x_ref, mod_ref, w_ref, g_ref, b_ref, o_ref):
    mix = jnp.concatenate([oa_ref[...], ob_ref[...], oc_ref[...], od_ref[...]], axis=-1)
    m = _dot(mix, w_ref[...])
    r = DEEPNORM_ALPHA * x_ref[...] + mod_ref[2:3, :] * m
    o_ref[...] = _layer_norm(r, g_ref[...], b_ref[...])


def _outproj_call(oa, ob, oc, od, x, mod, w_out, ln_g, ln_b, per_batch):
    B, n, _ = x.shape
    part = pl.BlockSpec((None, ROW_BLOCK, GROUP_W), lambda b, r: (b, r, 0))
    row = pl.BlockSpec((None, ROW_BLOCK, D_MODEL), lambda b, r: (b, r, 0))
    return pl.pallas_call(
        _outproj_kernel,
        grid=(B, n // ROW_BLOCK),
        in_specs=[part, part, part, part, row,
                  pl.BlockSpec((None, 6, D_MODEL), _mod_index(per_batch)),
                  _small((D_MODEL, D_MODEL)), _small((1, D_MODEL)), _small((1, D_MODEL))],
        out_specs=row,
        out_shape=jax.ShapeDtypeStruct((B, n, D_MODEL), F32),
        compiler_params=_params("parallel", "parallel"),
        name="out_proj_ln",
    )(oa, ob, oc, od, x, mod, w_out, ln_g, ln_b)


def _top_values(s, count):
    vals = []
    w = s
    for _ in range(count):
        m = jnp.max(w, axis=0, keepdims=True)
        vals.append(m)
        w = jnp.where(w >= m, -jnp.inf, w)
    return jnp.concatenate(vals, axis=0)


def _peer_route(s1, s2):
    k = PEER_TOPK
    a1 = _top_values(s1, k)
    a2 = _top_values(s2, k)
    cands = [a1[0:1, :] + a2] + [a1[i:i + 1, :] + a2[0:k // 2, :] for i in range(1, k)]
    kth = _top_values(jnp.concatenate(cands, axis=0), k)[k - 1:k, :]
    e1s = jnp.exp(a1 - a1[0:1, :])
    e2s = jnp.exp(a2 - a2[0:1, :])
    thr = jnp.full(s1.shape, jnp.inf, F32)
    zsum = jnp.zeros(kth.shape, F32)
    for i in range(k):
        sel = (a1[i:i + 1, :] + a2) >= kth
        thr_i = jnp.min(jnp.where(sel, a2, jnp.inf), axis=0, keepdims=True)
        zsum = zsum + e1s[i:i + 1, :] * jnp.sum(jnp.where(sel, e2s, 0.0), axis=0, keepdims=True)
        thr = jnp.where(s1 == a1[i:i + 1, :], thr_i, thr)
    e1 = jnp.exp(s1 - a1[0:1, :])
    e2n = jnp.exp(s2 - a2[0:1, :]) / zsum
    return e1, e2n, thr


def _peer_kernel(x_ref, mod_ref, wq_ref, kd_ref, u_ref, vt_ref, g_ref, b_ref, o_ref,
                 hb_ref, yt_ref, s2_ref, e2n_ref, thr_ref, e1_ref):
    j = pl.program_id(1)

    @pl.when(j == 0)
    def _():
        wh, wl = _split(wq_ref[...])
        kd = kd_ref[...]
        for t in range(PEER_SUB):
            rows = slice(t * PEER_LANES, (t + 1) * PEER_LANES)
            h = x_ref[rows, :] * (1.0 + mod_ref[4:5, :]) + mod_ref[3:4, :]
            hh, hl = _split(h)
            hb_ref[t] = hh
            yt_ref[t] = jnp.zeros((D_MODEL, PEER_LANES), F32)
            q = _dot(hh, wh) + _dot(hl, wh) + _dot(hh, wl)
            for hd in range(PEER_HEADS):
                s = _dot3(kd, q[:, hd * N_KEYS:(hd + 1) * N_KEYS], NT)
                e1, e2n, thr = _peer_route(s[0:N_KEYS, :], s[N_KEYS:, :])
                s2_ref[hd, t] = s[N_KEYS:, :]
                e2n_ref[hd, t] = e2n
                thr_ref[hd, t] = thr
                e1_ref[hd, t] = e1

    for t in range(PEER_SUB):
        act = _gelu(_dot(u_ref[...], hb_ref[t], NT))
        parts = []
        for r in range(PEER_ROWS):
            a = j * PEER_ROWS + r
            w = jnp.zeros((N_KEYS, PEER_LANES), F32)
            for hd in range(PEER_HEADS):
                thr_row = thr_ref[hd, t, pl.ds(a, 1), :]
                e1_row = e1_ref[hd, t, pl.ds(a, 1), :]
                w = w + jnp.where(s2_ref[hd, t] >= thr_row, e2n_ref[hd, t], 0.0) * e1_row
            parts.append((w * act[r * N_KEYS:(r + 1) * N_KEYS, :]).astype(BF16))
        yt_ref[t] = yt_ref[t] + _dot(vt_ref[...], jnp.concatenate(parts, axis=0))

    @pl.when(j == pl.num_programs(1) - 1)
    def _():
        for t in range(PEER_SUB):
            rows = slice(t * PEER_LANES, (t + 1) * PEER_LANES)
            r = DEEPNORM_ALPHA * x_ref[rows, :] + mod_ref[5:6, :] * yt_ref[t].T
            o_ref[rows, :] = _layer_norm(r, g_ref[...], b_ref[...])


def _peer_call(x, mod, wq, kd, u, vt, ln_g, ln_b, per_batch):
    B, n, _ = x.shape
    T = B * n
    if per_batch:
        mod_idx = lambda i, j: ((i * PEER_TOKENS) // n, 0, 0)
    else:
        mod_idx = lambda i, j: (0, 0, 0)
    stat = pltpu.VMEM((PEER_HEADS, PEER_SUB, N_KEYS, PEER_LANES), F32)
    out = pl.pallas_call(
        _peer_kernel,
        grid=(T // PEER_TOKENS, N_EXPERTS // PEER_TILE),
        in_specs=[
            pl.BlockSpec((PEER_TOKENS, D_MODEL), lambda i, j: (i, 0)),
            pl.BlockSpec((None, 6, D_MODEL), mod_idx),
            pl.BlockSpec((D_MODEL, D_MODEL), lambda i, j: (0, 0)),
            pl.BlockSpec((2 * N_KEYS, N_KEYS), lambda i, j: (0, 0)),
            pl.BlockSpec((PEER_TILE, D_MODEL), lambda i, j: (j, 0)),
            pl.BlockSpec((D_MODEL, PEER_TILE), lambda i, j: (0, j)),
            pl.BlockSpec((1, D_MODEL), lambda i, j: (0, 0)),
            pl.BlockSpec((1, D_MODEL), lambda i, j: (0, 0)),
        ],
        out_specs=pl.BlockSpec((PEER_TOKENS, D_MODEL), lambda i, j: (i, 0)),
        out_shape=jax.ShapeDtypeStruct((T, D_MODEL), F32),
        scratch_shapes=[pltpu.VMEM((PEER_SUB, PEER_LANES, D_MODEL), BF16),
                        pltpu.VMEM((PEER_SUB, D_MODEL, PEER_LANES), F32),
                        stat, stat, stat, stat],
        compiler_params=_params("parallel", "arbitrary"),
        name="peer_dense",
    )(x.reshape(T, D_MODEL), mod, wq, kd, u, vt, ln_g, ln_b)
    return out.reshape(B, n, D_MODEL)


def _lambda_init(l):
    return 0.8 - 0.6 * math.exp(-0.3 * l)


def _block_diag_states(s):
    B = s.shape[0]
    out = jnp.zeros((B, 2, A_HEADS, HEAD_W, A_HEADS, HEAD_W), F32)
    for h in range(A_HEADS):
        out = out.at[:, :, h, :, h, :].set(s[:, :, h])
    return out.reshape(B, 2, GROUP_W, GROUP_W)


def _diag_states(st):
    B = st.shape[0]
    s6 = st.reshape(B, 2, A_HEADS, HEAD_W, A_HEADS, HEAD_W)
    return jnp.stack([s6[:, :, h, :, h, :] for h in range(A_HEADS)], axis=2)


def _trunk_layer(x, mod, p, l, ctx=None):
    B, n, _ = x.shape
    per_batch = ctx is not None
    z = _inproj_call(x, mod, p['w_in'], per_batch)
    li = jnp.full((1, 128), _lambda_init(l), F32)
    if ctx is None:
        oa = _attn_ctx_call(z, p['diff_lambda'], li, p['subln'])
        oc, st = _ret_call(z, p['dec_rows'], p['dec_cols'])
    else:
        cache_k, cache_v, s0 = ctx
        oa = _attn_lat_call(z, cache_k, cache_v, l, p['diff_lambda'], li, p['subln'])
        oc, st = _ret_call(z, p['dec_rows'], p['dec_cols'], s0)
    ob = _gmlp_call(z, p['gmlp_ln_g'], p['gmlp_ln_b'], p['gmlp_ws'], p['gmlp_bias'])
    od = _conv_call(z, p['conv_w'], p['conv_b'], p['conv_ln_g'], p['conv_ln_b'])
    x1 = _outproj_call(oa, ob, oc, od, x, mod, p['w_out'], p['ln1_g'], p['ln1_b'], per_batch)
    x2 = _peer_call(x1, mod, p['peer_wq'], p['peer_kd'], p['peer_u'], p['peer_vt'],
                    p['ln2_g'], p['ln2_b'], per_batch)
    return x2, z, st


def kernel(x_prompt, x_sample, cache_k, cache_v, state_ret, c, c_ctx, w_mod, b_mod, w_in,
           diff_lambda, diff_subln_g, gmlp_ln_g, gmlp_ln_b, gmlp_ws, gmlp_bs, ret_decay,
           conv_w, conv_b, conv_ln_g, conv_ln_b, w_out, ln1_g, ln1_b, ln2_g, ln2_b,
           peer_wq, peer_keys, peer_u, peer_v):
    B = x_prompt.shape[0]
    Bd = x_sample.shape[0]
    n_ctx = x_prompt.shape[1]
    rows = ((Bd + 1 + 7) // 8) * 8
    cond = jnp.concatenate([c, c_ctx[None, :], jnp.zeros((rows - Bd - 1, D_MODEL), F32)], axis=0)
    mods = _mod_call(cond, w_mod, b_mod).reshape(DEPTH, rows, 6, D_MODEL)

    ck = cache_k.reshape(Bd, DEPTH, PAST_LEN, GROUP_W)
    cv = cache_v.reshape(Bd, DEPTH, PAST_LEN, GROUP_W)
    row = lambda a: a.reshape(1, -1)

    xp, xs = x_prompt, x_sample
    ks_out, vs_out, ss_out = [], [], []
    for l in range(DEPTH):
        zk = jnp.zeros((N_KEYS, N_KEYS // 2), F32)
        p = {
            'w_in': w_in[l].astype(BF16),
            'w_out': w_out[l].astype(BF16),
            'diff_lambda': diff_lambda[l],
            'subln': row(jnp.tile(diff_subln_g[l], A_HEADS)),
            'gmlp_ln_g': row(gmlp_ln_g[l]), 'gmlp_ln_b': row(gmlp_ln_b[l]),
            'gmlp_ws': gmlp_ws[l],
            'gmlp_bias': jnp.repeat(gmlp_bs[l].T, GROUP_W // gmlp_bs.shape[1], axis=1),
            'dec_rows': jnp.broadcast_to(ret_decay[l].reshape(8, 1), (8, 128)),
            'dec_cols': jnp.repeat(ret_decay[l], HEAD_W, axis=1),
            'conv_w': conv_w[l], 'conv_b': row(conv_b[l]),
            'conv_ln_g': row(conv_ln_g[l]), 'conv_ln_b': row(conv_ln_b[l]),
            'ln1_g': row(ln1_g[l]), 'ln1_b': row(ln1_b[l]),
            'ln2_g': row(ln2_g[l]), 'ln2_b': row(ln2_b[l]),
            'peer_wq': peer_wq[l],
            'peer_kd': jnp.concatenate(
                [jnp.concatenate([peer_keys[l, 0], zk], axis=1),
                 jnp.concatenate([zk, peer_keys[l, 1]], axis=1)], axis=0),
            'peer_u': peer_u[l].astype(BF16),
            'peer_vt': peer_v[l].astype(BF16).T,
        }
        xp, z, st = _trunk_layer(xp, mods[l, Bd:Bd + 1], p, l)
        ks_out.append(z[:, :, GROUP_W:2 * GROUP_W].reshape(B, n_ctx, A_HEADS, 2 * A_DH))
        vs_out.append(z[:, :, 2 * GROUP_W:3 * GROUP_W].reshape(B, n_ctx, A_HEADS, A_DV))
        ss_out.append(_diag_states(st))
        s0 = _block_diag_states(state_ret[:, l])
        xs, _, _ = _trunk_layer(xs, mods[l, :Bd], p, l, (ck, cv, s0))
    return (xp, xs, jnp.stack(ks_out, axis=1), jnp.stack(vs_out, axis=1),
            jnp.stack(ss_out, axis=1))
```
